```python
import jax, jax.numpy as jnp
from jax import lax
import numpy as np

D_MODEL = 2048
BATCH = 4
SEQ = 8192
DEPTH = 2
DEC_BATCH = 4
DEC_SEQ = 2048
PAST_LEN = 128

GRID_W = 64
N_META = 16
QBLK = 128
NA_HEADS = 4
NA_HD = 128
NA_KH_MAX = 8
NA_KW = 16
MLA_HEADS = 4
MLA_NOPE = 128
MLA_ROPE = 64
MLA_VD = 128
MLA_KV_RANK = 512
GQA_HEADS = 4
GQA_KV_HEADS = 2
GQA_HD = 128
ML_HEADS = 4
ML_HD = 128
ML_CHUNK = 128
ML_N_GATES = 4
N_BRANCH = 4
BRANCH_W = 512
N_EXPERTS = 32
TOP_K = 4
D_EXPERT = 2048
SWIGLU_LIMIT = 7.0
SWIGLU_ALPHA = 1.702
MOE_BLOCK = 256
ROPE_THETA = 10000.0
LN_EPS = 1e-5
RMS_EPS = 1e-6

IN_SPLITS = (
    NA_HEADS * NA_HD, NA_HEADS * NA_HD, NA_HEADS * NA_HD,
    MLA_HEADS * (MLA_NOPE + MLA_ROPE), MLA_KV_RANK, MLA_ROPE,
    GQA_HEADS * GQA_HD, GQA_KV_HEADS * GQA_HD, GQA_KV_HEADS * GQA_HD,
    ML_HEADS * ML_HD, ML_HEADS * ML_HD, ML_HEADS * ML_HD, ML_HEADS * ML_HD,
    ML_N_GATES * ML_HEADS,
    N_BRANCH * D_MODEL,
)
IN_COLS = sum(IN_SPLITS)

kernel_name = 'hybrid_bidir_encoder_na_mla_gqa_mlstm_moe'

F32 = jnp.float32


def layer_norm(x, g, b):
    xf = x.astype(F32)
    mu = jnp.mean(xf, -1, keepdims=True)
    var = jnp.mean(jnp.square(xf - mu), -1, keepdims=True)
    y = (xf - mu) * lax.rsqrt(var + LN_EPS) * g.astype(F32) + b.astype(F32)
    return y.astype(x.dtype)


def rms_norm(x, g):
    xf = x.astype(F32)
    y = xf * lax.rsqrt(jnp.mean(jnp.square(xf), -1, keepdims=True) + RMS_EPS) * g.astype(F32)
    return y.astype(x.dtype)


def inv_freq(n_dims):
    return 1.0 / (ROPE_THETA ** (jnp.arange(0, n_dims, 2, dtype=F32) / n_dims))


def apply_rope(x, ang):
    n = x.shape[-1] // 2
    cos = jnp.cos(ang)[:, None, :].astype(x.dtype)
    sin = jnp.sin(ang)[:, None, :].astype(x.dtype)
    x1, x2 = x[..., :n], x[..., n:]
    return jnp.concatenate([x1 * cos - x2 * sin, x2 * cos + x1 * sin], -1)


def axial_rope(x, ang_row, ang_col):
    half = x.shape[-1] // 2
    return jnp.concatenate([apply_rope(x[..., :half], ang_row), apply_rope(x[..., half:], ang_col)], -1)


def split_columns(p):
    parts, off = [], 0
    for w in IN_SPLITS:
        parts.append(p[..., off:off + w])
        off += w
    return parts


def blocked_attention(q, k, v, scale):
    B, L, KV, G, dq = q.shape
    T = L - N_META

    def attend(qb):
        s = jnp.einsum('bqkgd,bskd->bkgqs', qb, k).astype(F32) * scale
        p = jax.nn.softmax(s, axis=-1).astype(v.dtype)
        return jnp.einsum('bkgqs,bskd->bqkgd', p, v)

    out_meta = attend(q[:, :N_META])
    qr = jnp.moveaxis(q[:, N_META:].reshape(B, T // QBLK, QBLK, KV, G, dq), 1, 0)
    out_real = lax.map(attend, qr)
    out_real = jnp.moveaxis(out_real, 0, 1).reshape(B, T, KV, G, v.shape[-1])
    return jnp.concatenate([out_meta, out_real], axis=1)


def neighbourhood_attention(q, k, v, rpb):
    B, L, H, d = q.shape
    T = L - N_META
    rows = T // GRID_W
    kh = min(NA_KH_MAX, rows)
    scale = d ** -0.5
    qm, km, vm = q[:, :N_META], k[:, :N_META], v[:, :N_META]
    qg = q[:, N_META:].reshape(B, rows, GRID_W, H, d)
    kg = k[:, N_META:].reshape(B, rows, GRID_W, H, d)
    vg = v[:, N_META:].reshape(B, rows, GRID_W, H, d)
    cols = np.arange(GRID_W)
    col_start = np.clip(cols - NA_KW // 2, 0, GRID_W - NA_KW)
    col_idx = col_start[:, None] + np.arange(NA_KW)[None, :]
    dc_idx = jnp.asarray(col_idx - cols[:, None] + (NA_KW - 1))
    n_loc = kh * NA_KW

    def row_attend(r):
        rs = jnp.clip(r - kh // 2, 0, rows - kh)
        q_r = lax.dynamic_index_in_dim(qg, r, axis=1, keepdims=False)
        k_win = lax.dynamic_slice_in_dim(kg, rs, kh, axis=1)[:, :, col_idx]
        v_win = lax.dynamic_slice_in_dim(vg, rs, kh, axis=1)[:, :, col_idx]
        dr_idx = rs + jnp.arange(kh) - r + (NA_KH_MAX - 1)
        bias = rpb[:, dr_idx[None, :, None], dc_idx[:, None, :]]
        s_loc = jnp.einsum('bchd,bicjhd->bhcij', q_r, k_win).astype(F32) * scale + bias[None].astype(F32)
        s_meta = jnp.einsum('bchd,bmhd->bhcm', q_r, km).astype(F32) * scale
        s = jnp.concatenate([s_loc.reshape(B, H, GRID_W, n_loc), s_meta], -1)
        p = jax.nn.softmax(s, axis=-1).astype(v.dtype)
        p_loc = p[..., :n_loc].reshape(B, H, GRID_W, kh, NA_KW)
        p_meta = p[..., n_loc:]
        return (jnp.einsum('bhcij,bicjhd->bchd', p_loc, v_win)
                + jnp.einsum('bhcm,bmhd->bchd', p_meta, vm))

    out_g = lax.map(row_attend, jnp.arange(rows))
    out_g = jnp.moveaxis(out_g, 0, 1).reshape(B, T, H, d)
    s_mm = jnp.einsum('bqhd,bkhd->bhqk', qm, km).astype(F32) * scale
    out_m = jnp.einsum('bhqk,bkhd->bqhd', jax.nn.softmax(s_mm, axis=-1).astype(v.dtype), vm)
    return jnp.concatenate([out_m, out_g], axis=1)


def mla_attention(q, ckv, kr, kv_norm_g, w_ukv, ang):
    B, L, _ = q.shape
    q = q.reshape(B, L, MLA_HEADS, MLA_NOPE + MLA_ROPE)
    q = jnp.concatenate([q[..., :MLA_NOPE], apply_rope(q[..., MLA_NOPE:], ang)], -1)
    c = rms_norm(ckv, kv_norm_g)
    kv = (c @ w_ukv).reshape(B, L, MLA_HEADS, MLA_NOPE + MLA_VD)
    k_nope, v = kv[..., :MLA_NOPE], kv[..., MLA_NOPE:]
    k_r = apply_rope(kr[:, :, None, :], ang)
    k = jnp.concatenate([k_nope, jnp.broadcast_to(k_r, (B, L, MLA_HEADS, MLA_ROPE))], -1)
    out = blocked_attention(q[:, :, :, None, :], k, v, (MLA_NOPE + MLA_ROPE) ** -0.5)
    return out.reshape(B, L, MLA_HEADS * MLA_VD)


def gqa_attention(q, k, v, q_norm_g, k_norm_g, ang_row, ang_col):
    B, L, _ = q.shape
    g = GQA_HEADS // GQA_KV_HEADS
    q = axial_rope(rms_norm(q.reshape(B, L, GQA_HEADS, GQA_HD), q_norm_g), ang_row, ang_col)
    k = axial_rope(rms_norm(k.reshape(B, L, GQA_KV_HEADS, GQA_HD), k_norm_g), ang_row, ang_col)
    v = v.reshape(B, L, GQA_KV_HEADS, GQA_HD)
    out = blocked_attention(q.reshape(B, L, GQA_KV_HEADS, g, GQA_HD), k, v, GQA_HD ** -0.5)
    return out.reshape(B, L, GQA_HEADS * GQA_HD)


def mlstm_chunk(carry, inp):
    c_mat, n_vec, m_prev = carry
    q, k, v, li, lf = inp
    cl = q.shape[2]
    b = jnp.cumsum(lf, axis=-1)
    lower = jnp.tril(jnp.ones((cl, cl), dtype=bool))
    d_log = jnp.where(lower, b[..., :, None] - b[..., None, :] + li[..., None, :], -jnp.inf)
    m_inter = b + m_prev[..., None]
    m_t = jnp.maximum(m_inter, jnp.max(d_log, -1))
    w_inter = jnp.exp(m_inter - m_t)
    s = jnp.einsum('bhtd,bhsd->bhts', q, k) * jnp.exp(d_log - m_t[..., None])
    num = jnp.einsum('bhts,bhsv->bhtv', s, v) + w_inter[..., None] * jnp.einsum('bhtk,bhkv->bhtv', q, c_mat)
    den = jnp.sum(s, -1) + w_inter * jnp.einsum('bhtk,bhk->bht', q, n_vec)
    h = num / jnp.maximum(jnp.abs(den), jnp.exp(-m_t))[..., None]
    b_last = b[..., -1]
    w_log = b_last[..., None] - b + li
    m_new = jnp.maximum(b_last + m_prev, jnp.max(w_log, -1))
    decay = jnp.exp(b_last + m_prev - m_new)
    w_in = jnp.exp(w_log - m_new[..., None])
    c_new = decay[..., None, None] * c_mat + jnp.einsum('bhs,bhsk,bhsv->bhkv', w_in, k, v)
    n_new = decay[..., None] * n_vec + jnp.einsum('bhs,bhsk->bhk', w_in, k)
    return (c_new, n_new, m_new), h


def mlstm_scan(q, k, v, li, lf):
    B, H, L, dk = q.shape
    dv = v.shape[-1]
    T = L - N_META
    nc = T // ML_CHUNK
    carry = (jnp.zeros((B, H, dk, dv), F32), jnp.zeros((B, H, dk), F32), jnp.zeros((B, H), F32))
    carry, h_meta = mlstm_chunk(carry, (q[:, :, :N_META], k[:, :, :N_META], v[:, :, :N_META],
                                        li[:, :, :N_META], lf[:, :, :N_META]))

    def to_chunks(a):
        r = a[:, :, N_META:]
        r = r.reshape((B, H, nc, ML_CHUNK) + r.shape[3:])
        return jnp.moveaxis(r, 2, 0)

    _, h_real = lax.scan(mlstm_chunk, carry, (to_chunks(q), to_chunks(k), to_chunks(v),
                                              to_chunks(li), to_chunks(lf)))
    h_real = jnp.moveaxis(h_real, 0, 2).reshape(B, H, T, dv)
    return jnp.concatenate([h_meta, h_real], axis=2)


def flip_real(a):
    return jnp.concatenate([a[:, :, :N_META], jnp.flip(a[:, :, N_META:], axis=2)], axis=2)


def mlstm_mixer(q, k, v, o_pre, gate_pre, gate_bias, out_norm_g):
    B, L, _ = q.shape

    def heads(a):
        return a.reshape(B, L, ML_HEADS, ML_HD).transpose(0, 2, 1, 3).astype(F32)

    qh, kh, vh = heads(q), heads(k) * (ML_HD ** -0.5), heads(v)
    g = gate_pre.reshape(B, L, ML_N_GATES, ML_HEADS).astype(F32) + gate_bias.astype(F32)
    g = g.transpose(2, 0, 3, 1)
    li_f, lf_f = g[0], jax.nn.log_sigmoid(g[1])
    li_b, lf_b = g[2], jax.nn.log_sigmoid(g[3])
    h_fwd = mlstm_scan(qh, kh, vh, li_f, lf_f)
    h_bwd = flip_real(mlstm_scan(flip_real(qh), flip_real(kh), flip_real(vh), flip_real(li_b), flip_real(lf_b)))
    h = (h_fwd + h_bwd).transpose(0, 2, 1, 3)
    mu = jnp.mean(h, -1, keepdims=True)
    var = jnp.mean(jnp.square(h - mu), -1, keepdims=True)
    h = (h - mu) * lax.rsqrt(var + LN_EPS) * out_norm_g.reshape(ML_HEADS, ML_HD).astype(F32)
    o = jax.nn.sigmoid(o_pre).reshape(B, L, ML_HEADS, ML_HD)
    return (h.astype(q.dtype) * o).reshape(B, L, ML_HEADS * ML_HD)


def mixer_block(u, ang_seq, ang_row, ang_col, w_in, nat_rpb, mla_kv_norm, mla_w_ukv,
                gqa_q_norm, gqa_k_norm, mlstm_gate_bias, mlstm_out_norm, w_branch, w_out):
    B, L, _ = u.shape
    (na_q, na_k, na_v, mla_q, mla_ckv, mla_kr, g_q, g_k, g_v,
     m_q, m_k, m_v, m_o, m_g, gate_pre) = split_columns(u @ w_in)
    y_a = neighbourhood_attention(na_q.reshape(B, L, NA_HEADS, NA_HD), na_k.reshape(B, L, NA_HEADS, NA_HD),
                                  na_v.reshape(B, L, NA_HEADS, NA_HD), nat_rpb).reshape(B, L, BRANCH_W)
    y_b = mla_attention(mla_q, mla_ckv, mla_kr, mla_kv_norm, mla_w_ukv, ang_seq)
    y_c = gqa_attention(g_q, g_k, g_v, gqa_q_norm, gqa_k_norm, ang_row, ang_col)
    y_d = mlstm_mixer(m_q, m_k, m_v, m_o, m_g, mlstm_gate_bias, mlstm_out_norm)
    gates = jax.nn.sigmoid(gate_pre.reshape(B, L, N_BRANCH, D_MODEL))
    merged = gates[:, :, 0] * (y_a @ w_branch[0])
    merged = merged + gates[:, :, 1] * (y_b @ w_branch[1])
    merged = merged + gates[:, :, 2] * (y_c @ w_branch[2])
    merged = merged + gates[:, :, 3] * (y_d @ w_branch[3])
    return merged @ w_out


def moe_ffn(x, w_router, b_router, w_gate_up, b_gate_up, w_down, b_down):
    B, L, D = x.shape
    N = B * L
    xf = x.reshape(N, D)
    logits = (xf @ w_router).astype(F32) + b_router.astype(F32)
    top_v, top_i = lax.top_k(logits, TOP_K)
    gates = jax.nn.softmax(top_v, axis=-1)
    flat_e = top_i.reshape(-1)
    flat_tok = jnp.repeat(jnp.arange(N, dtype=jnp.int32), TOP_K)
    flat_w = gates.reshape(-1)
    order = jnp.argsort(flat_e)
    se, stok, sw = flat_e[order], flat_tok[order], flat_w[order]
    counts = jnp.bincount(flat_e, length=N_EXPERTS)
    padded = (counts + MOE_BLOCK - 1) // MOE_BLOCK * MOE_BLOCK
    pad_end = jnp.cumsum(padded)
    pad_start = pad_end - padded
    start = jnp.cumsum(counts) - counts
    dest = pad_start[se] + jnp.arange(N * TOP_K) - start[se]
    n_blocks = -(-(N * TOP_K) // MOE_BLOCK) + N_EXPERTS
    P = n_blocks * MOE_BLOCK
    rows = jnp.full((P,), N, jnp.int32).at[dest].set(stok)
    wts = jnp.zeros((P,), F32).at[dest].set(sw)
    block_e = jnp.minimum(jnp.searchsorted(pad_end, jnp.arange(n_blocks) * MOE_BLOCK, side='right'),
                          N_EXPERTS - 1)
    x_pad = jnp.concatenate([xf, jnp.zeros((1, D), xf.dtype)], axis=0)

    def run_block(args):
        rb, e = args
        xb = x_pad[rb]
        gu = xb @ w_gate_up[e] + b_gate_up[e]
        g, up = gu[:, :D_EXPERT], gu[:, D_EXPERT:]
        g = jnp.minimum(g, SWIGLU_LIMIT)
        up = jnp.clip(up, -SWIGLU_LIMIT, SWIGLU_LIMIT)
        hmid = (up + 1.0) * (g * jax.nn.sigmoid(SWIGLU_ALPHA * g))
        return hmid @ w_down[e] + b_down[e]

    ys = lax.map(run_block, (rows.reshape(n_blocks, MOE_BLOCK), block_e))
    ys = ys.reshape(P, D)
    out = jnp.zeros((N + 1, D), ys.dtype).at[rows].add(ys * wts[:, None].astype(ys.dtype))
    return out[:N].reshape(B, L, D).astype(x.dtype)


def encoder_trunk(x, p):
    B, T, _ = x.shape
    L = T + N_META
    meta = jnp.broadcast_to(p['meta_tokens'].astype(x.dtype)[None], (B, N_META, D_MODEL))
    h = layer_norm(jnp.concatenate([meta, x], axis=1), p['ln_emb_g'], p['ln_emb_b'])
    pos = jnp.arange(L, dtype=F32)
    ang_seq = pos[:, None] * inv_freq(MLA_ROPE)[None, :]
    t = jnp.arange(T)
    zeros_m = jnp.zeros((N_META,), F32)
    row = jnp.concatenate([zeros_m, (t // GRID_W).astype(F32)])
    col = jnp.concatenate([zeros_m, (t % GRID_W).astype(F32)])
    f_ax = inv_freq(GQA_HD // 2)
    ang_row = row[:, None] * f_ax[None, :]
    ang_col = col[:, None] * f_ax[None, :]
    alpha = (2.0 * DEPTH) ** 0.25
    for l in range(DEPTH):
        mix = mixer_block(h, ang_seq, ang_row, ang_col, p['w_in'][l], p['nat_rpb'][l],
                          p['mla_kv_norm'][l], p['mla_w_ukv'][l], p['gqa_q_norm'][l], p['gqa_k_norm'][l],
                          p['mlstm_gate_bias'][l], p['mlstm_out_norm'][l], p['w_branch'][l], p['w_out'][l])
        h = layer_norm(alpha * h + mix, p['ln1_g'][l], p['ln1_b'][l])
        ffn = moe_ffn(h, p['w_router'][l], p['b_router'][l], p['w_gate_up'][l], p['b_gate_up'][l],
                      p['w_down'][l], p['b_down'][l])
        h = layer_norm(alpha * h + ffn, p['ln2_g'][l], p['ln2_b'][l])
    return h[:, N_META:]


def setup_inputs(seed: int = 0) -> dict:
    key = jax.random.key(seed)
    ks = jax.random.split(key, 26)

    def nrm(k, shape, s):
        return jax.random.normal(k, shape, F32) * s

    beta = (8.0 * DEPTH) ** -0.25
    fb = jnp.linspace(3.0, 6.0, ML_HEADS, dtype=F32)
    zh = jnp.zeros((ML_HEADS,), F32)
    gate_bias_base = jnp.stack([zh, fb, zh, fb])
    return {
        'x_prompt': nrm(ks[0], (BATCH, SEQ, D_MODEL), 1.0),
        'x_sample': nrm(ks[1], (DEC_BATCH, DEC_SEQ, D_MODEL), 1.0),
        'meta_tokens': nrm(ks[2], (N_META, D_MODEL), 1.0),
        'ln_emb_g': 1.0 + nrm(ks[3], (D_MODEL,), 0.02),
        'ln_emb_b': nrm(ks[4], (D_MODEL,), 0.02),
        'w_in': nrm(ks[5], (DEPTH, D_MODEL, IN_COLS), D_MODEL ** -0.5),
        'nat_rpb': nrm(ks[6], (DEPTH, NA_HEADS, 2 * NA_KH_MAX - 1, 2 * NA_KW - 1), 0.1),
        'mla_kv_norm': 1.0 + nrm(ks[7], (DEPTH, MLA_KV_RANK), 0.02),
        'mla_w_ukv': nrm(ks[8], (DEPTH, MLA_KV_RANK, MLA_HEADS * (MLA_NOPE + MLA_VD)), MLA_KV_RANK ** -0.5),
        'gqa_q_norm': 1.0 + nrm(ks[9], (DEPTH, GQA_HD), 0.02),
        'gqa_k_norm': 1.0 + nrm(ks[10], (DEPTH, GQA_HD), 0.02),
        'mlstm_gate_bias': gate_bias_base[None] + nrm(ks[11], (DEPTH, ML_N_GATES, ML_HEADS), 0.1),
        'mlstm_out_norm': 1.0 + nrm(ks[12], (DEPTH, ML_HEADS * ML_HD), 0.02),
        'w_branch': nrm(ks[13], (DEPTH, N_BRANCH, BRANCH_W, D_MODEL), BRANCH_W ** -0.5 * beta),
        'w_out': nrm(ks[14], (DEPTH, D_MODEL, D_MODEL), D_MODEL ** -0.5 * beta),
        'ln1_g': 1.0 + nrm(ks[15], (DEPTH, D_MODEL), 0.02),
        'ln1_b': nrm(ks[16], (DEPTH, D_MODEL), 0.02),
        'w_router': nrm(ks[17], (DEPTH, D_MODEL, N_EXPERTS), D_MODEL ** -0.5),
        'b_router': nrm(ks[18], (DEPTH, N_EXPERTS), 0.01),
        'w_gate_up': nrm(ks[19], (DEPTH, N_EXPERTS, D_MODEL, 2 * D_EXPERT), D_MODEL ** -0.5),
        'b_gate_up': nrm(ks[20], (DEPTH, N_EXPERTS, 2 * D_EXPERT), 0.01),
        'w_down': nrm(ks[21], (DEPTH, N_EXPERTS, D_EXPERT, D_MODEL), D_EXPERT ** -0.5 * beta),
        'b_down': nrm(ks[22], (DEPTH, N_EXPERTS, D_MODEL), 0.01),
        'ln2_g': 1.0 + nrm(ks[23], (DEPTH, D_MODEL), 0.02),
        'ln2_b': nrm(ks[24], (DEPTH, D_MODEL), 0.02),
    }


def reference(x_prompt, x_sample, meta_tokens, ln_emb_g, ln_emb_b, w_in, nat_rpb, mla_kv_norm, mla_w_ukv,
              gqa_q_norm, gqa_k_norm, mlstm_gate_bias, mlstm_out_norm, w_branch, w_out, ln1_g, ln1_b,
              w_router, b_router, w_gate_up, b_gate_up, w_down, b_down, ln2_g, ln2_b):
    params = {
        'meta_tokens': meta_tokens, 'ln_emb_g': ln_emb_g, 'ln_emb_b': ln_emb_b, 'w_in': w_in,
        'nat_rpb': nat_rpb, 'mla_kv_norm': mla_kv_norm, 'mla_w_ukv': mla_w_ukv,
        'gqa_q_norm': gqa_q_norm, 'gqa_k_norm': gqa_k_norm, 'mlstm_gate_bias': mlstm_gate_bias,
        'mlstm_out_norm': mlstm_out_norm, 'w_branch': w_branch, 'w_out': w_out,
        'ln1_g': ln1_g, 'ln1_b': ln1_b, 'w_router': w_router, 'b_router': b_router,
        'w_gate_up': w_gate_up, 'b_gate_up': b_gate_up, 'w_down': w_down, 'b_down': b_down,
        'ln2_g': ln2_g, 'ln2_b': ln2_b,
    }
    y_prompt = encoder_trunk(x_prompt, params)
    y_sample = encoder_trunk(x_sample, params)
    return (y_prompt, y_sample)
```

```python
import functools
from typing import NamedTuple

import numpy as np
import jax
import jax.numpy as jnp
from jax import lax
from jax.experimental import pallas as pl
from jax.experimental.pallas import tpu as pltpu

F32 = jnp.float32
BF16 = jnp.bfloat16

N_META = 16
GRID_W = 64
NA_HEADS, NA_HD, NA_KH, NA_KW = 4, 128, 8, 16
MLA_HEADS, MLA_NOPE, MLA_ROPE, MLA_VD, MLA_KV_RANK = 4, 128, 64, 128, 512
GQA_HEADS, GQA_KV_HEADS, GQA_HD = 4, 2, 128
ML_HEADS, ML_HD, ML_CHUNK, ML_N_GATES = 4, 128, 128, 4
N_BRANCH, BRANCH_W = 4, 512
TOP_K = 4
SWIGLU_LIMIT, SWIGLU_ALPHA = 7.0, 1.702
ROPE_THETA, LN_EPS, RMS_EPS = 10000.0, 1e-5, 1e-6

LANES = 128
META_PAD = 128
MLA_QK_PAD = 256
NEG = -1e30
VMEM_LIMIT = 56 * 1024 * 1024

A_NA_Q, A_NA_K, A_NA_V, A_M_Q, A_M_K, A_M_V, A_G_V, A_COLS = 0, 512, 1024, 1536, 2048, 2560, 3072, 3584
F_MLA_Q, F_G_Q, F_M_O, F_CKV, F_G_K, F_MISC, F_GATES = 0, 1024, 1536, 2048, 2560, 2816, 3072
MISC_KR, MISC_MG = 0, 64


class Group(NamedTuple):
    batch: int
    seq: int
    base: int
    meta_base: int


def _cparams(sem, vmem=VMEM_LIMIT):
    return pltpu.CompilerParams(dimension_semantics=sem, vmem_limit_bytes=vmem)


def _dot(a, b):
    return jnp.dot(a, b, preferred_element_type=F32)


def _dot_nt(a, b):
    return lax.dot_general(a, b, (((1,), (1,)), ((), ())), preferred_element_type=F32)


def _layer_norm(x, g, b):
    mu = jnp.mean(x, -1, keepdims=True)
    xc = x - mu
    var = jnp.mean(xc * xc, -1, keepdims=True)
    return xc * lax.rsqrt(var + LN_EPS) * g + b


def _pick(n, options):
    for t in options:
        if n % t == 0:
            return t
    raise ValueError(f"no tile for {n} in {options}")


def _emb_ln_kernel(x_ref, g_ref, b_ref, o_ref, obf_ref):
    y = _layer_norm(x_ref[...], g_ref[...], b_ref[...])
    o_ref[...] = y
    obf_ref[...] = y.astype(BF16)


def emb_layer_norm(x, g, b):
    R, D = x.shape
    tm = _pick(R, (256, 128))
    row = pl.BlockSpec((tm, D), lambda i: (i, 0))
    vec = pl.BlockSpec((1, D), lambda i: (0, 0))
    return pl.pallas_call(
        _emb_ln_kernel, grid=(R // tm,), in_specs=[row, vec, vec], out_specs=[row, row],
        out_shape=[jax.ShapeDtypeStruct((R, D), F32), jax.ShapeDtypeStruct((R, D), BF16)],
        compiler_params=_cparams(("parallel",)), name="emb_ln",
    )(x, g.reshape(1, D), b.reshape(1, D))


def _mm_kernel(x_ref, w_ref, o_ref):
    o_ref[...] = _dot(x_ref[...], w_ref[...]).astype(o_ref.dtype)


def matmul(x, w, out_dtype, name):
    M, K = x.shape
    N = w.shape[1]
    tm = _pick(M, (1024, 512, 256, 128))
    tn = _pick(N, (1024, 512, 256, 128))
    return pl.pallas_call(
        _mm_kernel, grid=(M // tm, N // tn),
        in_specs=[pl.BlockSpec((tm, K), lambda i, j: (i, 0)), pl.BlockSpec((K, tn), lambda i, j: (0, j))],
        out_specs=pl.BlockSpec((tm, tn), lambda i, j: (i, j)),
        out_shape=jax.ShapeDtypeStruct((M, N), out_dtype),
        compiler_params=_cparams(("parallel", "parallel")), name=name,
    )(x, w)


def _rope128(x, c, s):
    lane = lax.broadcasted_iota(jnp.int32, x.shape, 1)
    partner = jnp.where((lane & 32) == 0, pltpu.roll(x, 96, 1), pltpu.roll(x, 32, 1))
    return x * c + partner * s


def _rms_norm(x, g):
    return x * lax.rsqrt(jnp.mean(x * x, -1, keepdims=True) + RMS_EPS) * g


def _gqa_prep_kernel(q_ref, k_ref, c_ref, s_ref, gq_ref, gk_ref, qo_ref, ko_ref):
    c, s = c_ref[...], s_ref[...]
    for h in range(GQA_HEADS):
        sl = slice(h * GQA_HD, (h + 1) * GQA_HD)
        qo_ref[:, sl] = _rope128(_rms_norm(q_ref[:, sl], gq_ref[...]), c, s).astype(BF16)
    for h in range(GQA_KV_HEADS):
        sl = slice(h * GQA_HD, (h + 1) * GQA_HD)
        ko_ref[:, sl] = _rope128(_rms_norm(k_ref[:, sl], gk_ref[...]), c, s).astype(BF16)


def gqa_prep(proj_f, cos_t, sin_t, gq, gk):
    R = proj_f.shape[0]
    tm = _pick(R, (512, 256, 128))
    qw, kw = GQA_HEADS * GQA_HD, GQA_KV_HEADS * GQA_HD
    return pl.pallas_call(
        _gqa_prep_kernel, grid=(R // tm,),
        in_specs=[pl.BlockSpec((tm, qw), lambda i: (i, F_G_Q // qw)),
                  pl.BlockSpec((tm, kw), lambda i: (i, F_G_K // kw)),
                  pl.BlockSpec((tm, LANES), lambda i: (i, 0)),
                  pl.BlockSpec((tm, LANES), lambda i: (i, 0)),
                  pl.BlockSpec((1, LANES), lambda i: (0, 0)),
                  pl.BlockSpec((1, LANES), lambda i: (0, 0))],
        out_specs=[pl.BlockSpec((tm, qw), lambda i: (i, 0)), pl.BlockSpec((tm, kw), lambda i: (i, 0))],
        out_shape=[jax.ShapeDtypeStruct((R, qw), BF16), jax.ShapeDtypeStruct((R, kw), BF16)],
        compiler_params=_cparams(("parallel",)), name="gqa_prep",
    )(proj_f, proj_f, cos_t, sin_t, gq.reshape(1, LANES), gk.reshape(1, LANES))


def _mla_prep_kernel(q_ref, ckv_ref, misc_ref, c_ref, s_ref, g_ref, qo_ref, ck_ref):
    c, s = c_ref[...], s_ref[...]
    for h in range(MLA_HEADS):
        nope = slice(h * MLA_QK_PAD, h * MLA_QK_PAD + MLA_NOPE)
        rope = slice(h * MLA_QK_PAD + MLA_NOPE, (h + 1) * MLA_QK_PAD)
        qo_ref[:, nope] = q_ref[:, nope].astype(BF16)
        qo_ref[:, rope] = _rope128(q_ref[:, rope], c, s).astype(BF16)
    ck_ref[:, :MLA_KV_RANK] = _rms_norm(ckv_ref[...], g_ref[...]).astype(BF16)
    misc = misc_ref[...]
    lane = lax.broadcasted_iota(jnp.int32, misc.shape, 1)
    kr = jnp.where(lane < MLA_ROPE, _rope128(misc, c, s), 0.0)
    ck_ref[:, MLA_KV_RANK:] = kr.astype(BF16)


def mla_prep(proj_f, cos_t, sin_t, g):
    R = proj_f.shape[0]
    tm = _pick(R, (512, 256, 128))
    qw = MLA_HEADS * MLA_QK_PAD
    ckw = MLA_KV_RANK + LANES
    return pl.pallas_call(
        _mla_prep_kernel, grid=(R // tm,),
        in_specs=[pl.BlockSpec((tm, qw), lambda i: (i, F_MLA_Q // qw)),
                  pl.BlockSpec((tm, MLA_KV_RANK), lambda i: (i, F_CKV // MLA_KV_RANK)),
                  pl.BlockSpec((tm, LANES), lambda i: (i, F_MISC // LANES)),
                  pl.BlockSpec((tm, LANES), lambda i: (i, 0)),
                  pl.BlockSpec((tm, LANES), lambda i: (i, 0)),
                  pl.BlockSpec((1, MLA_KV_RANK), lambda i: (0, 0))],
        out_specs=[pl.BlockSpec((tm, qw), lambda i: (i, 0)), pl.BlockSpec((tm, ckw), lambda i: (i, 0))],
        out_shape=[jax.ShapeDtypeStruct((R, qw), BF16), jax.ShapeDtypeStruct((R, ckw), BF16)],
        compiler_params=_cparams(("parallel",)), name="mla_prep",
    )(proj_f, proj_f, proj_f, cos_t, sin_t, g.reshape(1, MLA_KV_RANK))


def _flash_kernel(q_ref, k_ref, v_ref, km_ref, vm_ref, o_ref, m_s, l_s, acc_s, *, heads, kv_heads, dq, dv, scale):
    ki = pl.program_id(2)
    group = heads // kv_heads

    def scores(h, kref):
        kv = h // group
        return _dot_nt(q_ref[:, h * dq:(h + 1) * dq], kref[:, kv * dq:(kv + 1) * dq]) * scale

    @pl.when(ki == 0)
    def _():
        for h in range(heads):
            kv = h // group
            s = scores(h, km_ref)
            col = lax.broadcasted_iota(jnp.int32, s.shape, 1)
            s = jnp.where(col < N_META, s, NEG)
            m = jnp.max(s, -1, keepdims=True)
            p = jnp.exp(s - m)
            m_s[h] = m
            l_s[h] = jnp.sum(p, -1, keepdims=True)
            acc_s[h] = _dot(p.astype(BF16), vm_ref[:, kv * dv:(kv + 1) * dv])

    for h in range(heads):
        kv = h // group
        s = scores(h, k_ref)
        m_prev = m_s[h]
        m_new = jnp.maximum(m_prev, jnp.max(s, -1, keepdims=True))
        alpha = jnp.exp(m_prev - m_new)
        p = jnp.exp(s - m_new)
        l_s[h] = alpha * l_s[h] + jnp.sum(p, -1, keepdims=True)
        acc_s[h] = alpha * acc_s[h] + _dot(p.astype(BF16), v_ref[:, kv * dv:(kv + 1) * dv])
        m_s[h] = m_new

    @pl.when(ki == pl.num_programs(2) - 1)
    def _():
        for h in range(heads):
            o_ref[:, h * dv:(h + 1) * dv] = (acc_s[h] / l_s[h]).astype(o_ref.dtype)


def _flash_call(q, qcb, k, kcb, v, vcb, grp, *, meta_queries, heads, kv_heads, dq, dv, scale, name):
    B, T = grp.batch, grp.seq
    tk = _pick(T, (512, 256, 128))
    tq = META_PAD if meta_queries else tk
    nq = 1 if meta_queries else T // tq
    nk = T // tk
    qw, kw, vw = heads * dq, kv_heads * dq, kv_heads * dv
    if meta_queries:
        q_map = lambda b, qi, ki: (grp.meta_base // META_PAD + b, qcb)
    else:
        q_map = lambda b, qi, ki: (grp.base // tq + b * nq + qi, qcb)
    kv_row = lambda b, ki: grp.base // tk + b * nk + ki
    meta_row = lambda b: grp.meta_base // META_PAD + b
    kern = functools.partial(_flash_kernel, heads=heads, kv_heads=kv_heads, dq=dq, dv=dv, scale=scale)
    return pl.pallas_call(
        kern, grid=(B, nq, nk),
        in_specs=[pl.BlockSpec((tq, qw), q_map),
                  pl.BlockSpec((tk, kw), lambda b, qi, ki: (kv_row(b, ki), kcb)),
                  pl.BlockSpec((tk, vw), lambda b, qi, ki: (kv_row(b, ki), vcb)),
                  pl.BlockSpec((META_PAD, kw), lambda b, qi, ki: (meta_row(b), kcb)),
                  pl.BlockSpec((META_PAD, vw), lambda b, qi, ki: (meta_row(b), vcb))],
        out_specs=pl.BlockSpec((tq, heads * dv), lambda b, qi, ki: (b * nq + qi, 0)),
        out_shape=jax.ShapeDtypeStruct((B * nq * tq, heads * dv), BF16),
        scratch_shapes=[pltpu.VMEM((heads, tq, 1), F32), pltpu.VMEM((heads, tq, 1), F32),
                        pltpu.VMEM((heads, tq, dv), F32)],
        compiler_params=_cparams(("parallel", "parallel", "arbitrary")), name=name,
    )(q, k, v, k, v)


def dense_attention(q, qcb, k, kcb, v, vcb, groups, **kw):
    assert all(g.base % g.seq == 0 for g in groups)
    name = kw.pop("name")
    real = [_flash_call(q, qcb, k, kcb, v, vcb, g, meta_queries=False, name=name + "_real", **kw) for g in groups]
    meta = [_flash_call(q, qcb, k, kcb, v, vcb, g, meta_queries=True, name=name + "_meta", **kw) for g in groups]
    return jnp.concatenate(real + meta, axis=0)


def _na_kernel(q_ref, k_ref, v_ref, qm_ref, km_ref, vm_ref, bias_ref, o_ref, om_ref):
    r = pl.program_id(1)
    rows = pl.num_programs(1)
    rs = jnp.clip(r - NA_KH // 2, 0, rows - NA_KH)
    d = r - rs
    start = pl.multiple_of(rs * GRID_W, GRID_W)
    win = NA_KH * GRID_W
    scale = NA_HD ** -0.5

    def meta_scores(q, h):
        s = _dot_nt(q, km_ref[:, h * NA_HD:(h + 1) * NA_HD]) * scale
        col = lax.broadcasted_iota(jnp.int32, s.shape, 1)
        return jnp.where(col < N_META, s, NEG)

    for h in range(NA_HEADS):
        sl = slice(h * NA_HD, (h + 1) * NA_HD)
        q = q_ref[:, sl]
        s = _dot_nt(q, k_ref[pl.ds(start, win), sl]) * scale + bias_ref[d, h]
        sm = meta_scores(q, h)
        m = jnp.maximum(jnp.max(s, -1, keepdims=True), jnp.max(sm, -1, keepdims=True))
        p = jnp.exp(s - m)
        pm = jnp.exp(sm - m)
        l = jnp.sum(p, -1, keepdims=True) + jnp.sum(pm, -1, keepdims=True)
        o = _dot(p.astype(BF16), v_ref[pl.ds(start, win), sl]) + _dot(pm.astype(BF16), vm_ref[:, sl])
        o_ref[:, sl] = (o / l).astype(o_ref.dtype)

    @pl.when(r == 0)
    def _():
        for h in range(NA_HEADS):
            sl = slice(h * NA_HD, (h + 1) * NA_HD)
            sm = meta_scores(qm_ref[:, sl], h)
            pm = jnp.exp(sm - jnp.max(sm, -1, keepdims=True))
            o = _dot(pm.astype(BF16), vm_ref[:, sl]) / jnp.sum(pm, -1, keepdims=True)
            om_ref[:, sl] = o.astype(om_ref.dtype)


def _na_bias_table(rpb):
    cols = np.arange(GRID_W)
    col_start = np.clip(cols - NA_KW // 2, 0, GRID_W - NA_KW)
    kc = np.arange(GRID_W)
    in_win = (kc[None, :] >= col_start[:, None]) & (kc[None, :] < col_start[:, None] + NA_KW)
    dc = np.clip(kc[None, :] - cols[:, None] + (NA_KW - 1), 0, 2 * NA_KW - 2)
    dr = np.arange(NA_KH)[None, :] - np.arange(NA_KH)[:, None] + (NA_KH - 1)
    tab = rpb[:, dr[:, None, :, None], dc[None, :, None, :]]
    tab = jnp.where(in_win[None, None, :, None, :], tab.astype(F32), NEG)
    return jnp.transpose(tab, (1, 0, 2, 3, 4)).reshape(NA_KH, NA_HEADS, GRID_W, NA_KH * GRID_W)


def neighbourhood_attention(proj_a, bias, groups):
    w = NA_HEADS * NA_HD
    real, meta = [], []
    for g in groups:
        B, T = g.batch, g.seq
        rows = T // GRID_W
        assert rows >= NA_KH and g.base % T == 0
        seq_blk = lambda cb: pl.BlockSpec((T, w), lambda b, r, cb=cb: (g.base // T + b, cb))
        meta_blk = lambda cb: pl.BlockSpec((META_PAD, w), lambda b, r, cb=cb: (g.meta_base // META_PAD + b, cb))
        o, om = pl.pallas_call(
            _na_kernel, grid=(B, rows),
            in_specs=[pl.BlockSpec((GRID_W, w), lambda b, r: (g.base // GRID_W + b * rows + r, A_NA_Q // w)),
                      seq_blk(A_NA_K // w), seq_blk(A_NA_V // w),
                      meta_blk(A_NA_Q // w), meta_blk(A_NA_K // w), meta_blk(A_NA_V // w),
                      pl.BlockSpec(bias.shape, lambda b, r: (0, 0, 0, 0))],
            out_specs=[pl.BlockSpec((GRID_W, w), lambda b, r: (b * rows + r, 0)),
                       pl.BlockSpec((META_PAD, w), lambda b, r: (b, 0))],
            out_shape=[jax.ShapeDtypeStruct((B * T, w), BF16), jax.ShapeDtypeStruct((B * META_PAD, w), BF16)],
            compiler_params=_cparams(("parallel", "arbitrary")), name="na_attn",
        )(proj_a, proj_a, proj_a, proj_a, proj_a, proj_a, bias)
        real.append(o)
        meta.append(om)
    return jnp.concatenate(real + meta, axis=0)


def _log_sigmoid(x):
    return jnp.minimum(x, 0.0) - jnp.log(1.0 + jnp.exp(-jnp.abs(x)))


def _mlstm_chunk(q, k, v, li_r, lf_r, li_c, lf_c, c_mat, n_vec, m_prev, *, reverse):
    cl = q.shape[0]
    t_idx = lax.broadcasted_iota(jnp.int32, (cl, cl), 0)
    s_idx = lax.broadcasted_iota(jnp.int32, (cl, cl), 1)
    seen = (s_idx >= t_idx) if reverse else (s_idx <= t_idx)
    seen_t = (t_idx >= s_idx) if reverse else (t_idx <= s_idx)
    b_col = jnp.sum(jnp.where(seen, lf_r, 0.0), 1, keepdims=True)
    b_row = jnp.sum(jnp.where(seen_t, lf_c, 0.0), 0, keepdims=True)
    total = jnp.sum(lf_r, 1, keepdims=True)
    d_log = jnp.where(seen, b_col - b_row + li_r, NEG)
    m_inter = b_col + m_prev
    m_t = jnp.maximum(m_inter, jnp.max(d_log, 1, keepdims=True))
    w_inter = jnp.exp(m_inter - m_t)
    scale = ML_HD ** -0.5
    s_mat = _dot_nt(q, k) * (scale * jnp.exp(d_log - m_t))
    num = _dot(s_mat.astype(BF16), v) + w_inter * _dot(q, c_mat.astype(BF16))
    qn = jnp.sum(q.astype(F32) * n_vec, 1, keepdims=True)
    den = jnp.sum(s_mat, 1, keepdims=True) + w_inter * qn
    h = num / jnp.maximum(jnp.abs(den), jnp.exp(-m_t))
    w_log_r = total - b_row + li_r
    w_log_c = total - b_col + li_c
    m_new = jnp.maximum(total + m_prev, jnp.max(w_log_r, 1, keepdims=True))
    decay = jnp.exp(total + m_prev - m_new)
    kw = k.astype(F32) * (jnp.exp(w_log_c - m_new) * scale)
    c_new = decay * c_mat + _dot(kw.T.astype(BF16), v)
    n_new = decay * n_vec + jnp.sum(kw, 0, keepdims=True)
    return h, c_new, n_new, m_new


def _mlstm_kernel(qf_ref, kf_ref, vf_ref, qb_ref, kb_ref, vb_ref, qm_ref, km_ref, vm_ref,
                  grf_ref, grb_ref, grm_ref, gcf_ref, gcb_ref, gcm_ref, bias_c_ref, bias_r_ref,
                  hf_ref, hb_ref, hmf_ref, hmb_ref, c_s, n_s, m_s):
    c = pl.program_id(1)

    def gates(gr_ref, gc_ref, direction, h, valid=None):
        gi, gf = 2 * direction * ML_HEADS + h, (2 * direction + 1) * ML_HEADS + h
        gr = gr_ref[...] + bias_c_ref[...]
        gc = gc_ref[...] + bias_r_ref[...]
        li_r, lf_r = gr[gi:gi + 1, :], _log_sigmoid(gr[gf:gf + 1, :])
        li_c, lf_c = gc[:, gi:gi + 1], _log_sigmoid(gc[:, gf:gf + 1])
        if valid is not None:
            pr = lax.broadcasted_iota(jnp.int32, li_r.shape, 1) < valid
            pc = lax.broadcasted_iota(jnp.int32, li_c.shape, 0) < valid
            li_r, lf_r = jnp.where(pr, li_r, NEG), jnp.where(pr, lf_r, 0.0)
            li_c, lf_c = jnp.where(pc, li_c, NEG), jnp.where(pc, lf_c, 0.0)
        return li_r, lf_r, li_c, lf_c

    def step(direction, h, q_ref, k_ref, v_ref, gr_ref, gc_ref, out_ref, *, reverse, valid=None):
        sl = slice(h * ML_HD, (h + 1) * ML_HD)
        hh, c_new, n_new, m_new = _mlstm_chunk(
            q_ref[:, sl], k_ref[:, sl], v_ref[:, sl], *gates(gr_ref, gc_ref, direction, h, valid),
            c_s[direction, h], n_s[direction, h], m_s[direction, h], reverse=reverse)
        out_ref[:, sl] = hh
        c_s[direction, h] = c_new
        n_s[direction, h] = n_new
        m_s[direction, h] = m_new

    @pl.when(c == 0)
    def _():
        c_s[...] = jnp.zeros_like(c_s)
        n_s[...] = jnp.zeros_like(n_s)
        m_s[...] = jnp.zeros_like(m_s)
        for h in range(ML_HEADS):
            step(0, h, qm_ref, km_ref, vm_ref, grm_ref, gcm_ref, hmf_ref, reverse=False, valid=N_META)
            step(1, h, qm_ref, km_ref, vm_ref, grm_ref, gcm_ref, hmb_ref, reverse=False, valid=N_META)

    for h in range(ML_HEADS):
        step(0, h, qf_ref, kf_ref, vf_ref, grf_ref, gcf_ref, hf_ref, reverse=False)
        step(1, h, qb_ref, kb_ref, vb_ref, grb_ref, gcb_ref, hb_ref, reverse=True)


def mlstm_scan(proj_a, gate_rows, gate_cols, gate_bias, groups):
    w = ML_HEADS * ML_HD
    ng = ML_N_GATES * ML_HEADS
    bias_c = gate_bias.reshape(ng, 1).astype(F32)
    bias_r = gate_bias.reshape(1, ng).astype(F32)
    outs = []
    for g in groups:
        B, T = g.batch, g.seq
        nc = T // ML_CHUNK
        assert g.base % ML_CHUNK == 0
        fwd = lambda b, c: g.base // ML_CHUNK + b * nc + c
        bwd = lambda b, c: g.base // ML_CHUNK + b * nc + (nc - 1 - c)
        met = lambda b, c: g.meta_base // META_PAD + b
        blk = lambda row, cb: pl.BlockSpec((ML_CHUNK, w), lambda b, c, row=row, cb=cb: (row(b, c), cb))
        grow = lambda row: pl.BlockSpec((ng, ML_CHUNK), lambda b, c, row=row: (0, row(b, c)))
        gcol = lambda row: pl.BlockSpec((ML_CHUNK, ng), lambda b, c, row=row: (row(b, c), 0))
        qkv = lambda row: [blk(row, A_M_Q // w), blk(row, A_M_K // w), blk(row, A_M_V // w)]
        res = pl.pallas_call(
            _mlstm_kernel, grid=(B, nc),
            in_specs=qkv(fwd) + qkv(bwd) + qkv(met)
            + [grow(fwd), grow(bwd), grow(met), gcol(fwd), gcol(bwd), gcol(met),
               pl.BlockSpec((ng, 1), lambda b, c: (0, 0)), pl.BlockSpec((1, ng), lambda b, c: (0, 0))],
            out_specs=[pl.BlockSpec((ML_CHUNK, w), lambda b, c: (b * nc + c, 0)),
                       pl.BlockSpec((ML_CHUNK, w), lambda b, c: (b * nc + (nc - 1 - c), 0)),
                       pl.BlockSpec((META_PAD, w), lambda b, c: (b, 0)),
                       pl.BlockSpec((META_PAD, w), lambda b, c: (b, 0))],
            out_shape=[jax.ShapeDtypeStruct((B * T, w), F32), jax.ShapeDtypeStruct((B * T, w), F32),
                       jax.ShapeDtypeStruct((B * META_PAD, w), F32), jax.ShapeDtypeStruct((B * META_PAD, w), F32)],
            scratch_shapes=[pltpu.VMEM((2, ML_HEADS, ML_HD, ML_HD), F32), pltpu.VMEM((2, ML_HEADS, 1, ML_HD), F32),
                            pltpu.VMEM((2, ML_HEADS, 1, 1), F32)],
            compiler_params=_cparams(("parallel", "arbitrary")), name="mlstm",
        )(*([proj_a] * 9), gate_rows, gate_rows, gate_rows, gate_cols, gate_cols, gate_cols, bias_c, bias_r)
        outs.append(res)
    hf = jnp.concatenate([o[0] for o in outs] + [o[2] for o in outs], axis=0)
    hb = jnp.concatenate([o[1] for o in outs] + [o[3] for o in outs], axis=0)
    return hf, hb


def _merge_kernel(ya_ref, yb_ref, yc_ref, hf_ref, hb_ref, o_ref, ng_ref, g0_ref, g1_ref, g2_ref, g3_ref,
                  wb_ref, out_ref):
    hsum = hf_ref[...] + hb_ref[...]
    parts = []
    for h in range(ML_HEADS):
        x = hsum[:, h * ML_HD:(h + 1) * ML_HD]
        mu = jnp.mean(x, -1, keepdims=True)
        xc = x - mu
        var = jnp.mean(xc * xc, -1, keepdims=True)
        parts.append(xc * lax.rsqrt(var + LN_EPS) * ng_ref[:, h * ML_HD:(h + 1) * ML_HD])
    hn = jnp.concatenate(parts, axis=-1)
    yd = (hn * jax.nn.sigmoid(o_ref[...])).astype(BF16)
    acc = jax.nn.sigmoid(g0_ref[...]) * _dot(ya_ref[...], wb_ref[0])
    acc += jax.nn.sigmoid(g1_ref[...]) * _dot(yb_ref[...], wb_ref[1])
    acc += jax.nn.sigmoid(g2_ref[...]) * _dot(yc_ref[...], wb_ref[2])
    acc += jax.nn.sigmoid(g3_ref[...]) * _dot(yd, wb_ref[3])
    out_ref[...] = acc.astype(out_ref.dtype)


def merge_branches(ya, yb, yc, hf, hb, proj_f, norm_g, w_branch):
    R = ya.shape[0]
    D = w_branch.shape[-1]
    tm = _pick(R, (512, 256, 128))
    tn = _pick(D, (512, 256, 128))
    nj = D // tn
    rowblk = pl.BlockSpec((tm, BRANCH_W), lambda i, j: (i, 0))
    gate = lambda br: pl.BlockSpec((tm, tn), lambda i, j, br=br: (i, F_GATES // tn + br * nj + j))
    return pl.pallas_call(
        _merge_kernel, grid=(R // tm, nj),
        in_specs=[rowblk, rowblk, rowblk, rowblk, rowblk,
                  pl.BlockSpec((tm, BRANCH_W), lambda i, j: (i, F_M_O // BRANCH_W)),
                  pl.BlockSpec((1, BRANCH_W), lambda i, j: (0, 0)),
                  gate(0), gate(1), gate(2), gate(3),
                  pl.BlockSpec((N_BRANCH, BRANCH_W, tn), lambda i, j: (0, 0, j))],
        out_specs=pl.BlockSpec((tm, tn), lambda i, j: (i, j)),
        out_shape=jax.ShapeDtypeStruct((R, D), BF16),
        compiler_params=_cparams(("parallel", "arbitrary")), name="merge",
    )(ya, yb, yc, hf, hb, proj_f, norm_g.reshape(1, BRANCH_W), proj_f, proj_f, proj_f, proj_f, w_branch)


def _out_proj_kernel(x_ref, w_ref, h_ref, g_ref, b_ref, o_ref, obf_ref, *, alpha):
    y = _layer_norm(alpha * h_ref[...] + _dot(x_ref[...], w_ref[...]), g_ref[...], b_ref[...])
    o_ref[...] = y
    obf_ref[...] = y.astype(BF16)


def out_proj_ln(merged, w_out, h, g, b, alpha):
    R, D = h.shape
    tm = _pick(R, (256, 128))
    row = pl.BlockSpec((tm, D), lambda i: (i, 0))
    vec = pl.BlockSpec((1, D), lambda i: (0, 0))
    return pl.pallas_call(
        functools.partial(_out_proj_kernel, alpha=alpha), grid=(R // tm,),
        in_specs=[row, pl.BlockSpec((D, D), lambda i: (0, 0)), row, vec, vec],
        out_specs=[row, row],
        out_shape=[jax.ShapeDtypeStruct((R, D), F32), jax.ShapeDtypeStruct((R, D), BF16)],
        compiler_params=_cparams(("parallel",)), name="out_proj_ln",
    )(merged, w_out, h, g.reshape(1, D), b.reshape(1, D))


def _router_kernel(x_ref, w_ref, b_ref, wt_ref, id_ref, *, n_experts):
    logits = _dot(x_ref[...], w_ref[...]) + b_ref[...]
    lane = lax.broadcasted_iota(jnp.int32, logits.shape, 1)
    lane_f = lane.astype(F32)
    cur = jnp.where(lane < n_experts, logits, NEG)
    vals, ids = [], []
    for _ in range(TOP_K):
        m = jnp.max(cur, -1, keepdims=True)
        idx = jnp.min(jnp.where(cur == m, lane_f, float(LANES)), -1, keepdims=True)
        vals.append(m)
        ids.append(idx)
        cur = jnp.where(lane_f == idx, NEG, cur)
    es = [jnp.exp(v - vals[0]) for v in vals]
    tot = es[0] + es[1] + es[2] + es[3]
    wt = jnp.zeros(logits.shape, F32)
    ix = jnp.zeros(logits.shape, F32)
    for k in range(TOP_K):
        wt = jnp.where(lane == k, es[k] / tot, wt)
        ix = jnp.where(lane == k, ids[k], ix)
    wt_ref[...] = wt
    id_ref[...] = ix.astype(jnp.int32)


def router(x_bf, w_router, b_router):
    R, D = x_bf.shape
    E = w_router.shape[1]
    tm = _pick(R, (512, 256, 128))
    w = jnp.zeros((D, LANES), BF16).at[:, :E].set(w_router.astype(BF16))
    b = jnp.zeros((1, LANES), F32).at[0, :E].set(b_router.astype(F32))
    return pl.pallas_call(
        functools.partial(_router_kernel, n_experts=E), grid=(R // tm,),
        in_specs=[pl.BlockSpec((tm, D), lambda i: (i, 0)), pl.BlockSpec((D, LANES), lambda i: (0, 0)),
                  pl.BlockSpec((1, LANES), lambda i: (0, 0))],
        out_specs=[pl.BlockSpec((tm, LANES), lambda i: (i, 0)), pl.BlockSpec((tm, LANES), lambda i: (i, 0))],
        out_shape=[jax.ShapeDtypeStruct((R, LANES), F32), jax.ShapeDtypeStruct((R, LANES), jnp.int32)],
        compiler_params=_cparams(("parallel",)), name="router",
    )(x_bf, w, b)


def _gather_kernel(nused_ref, idx_ref, x_hbm, o_ref, buf, sem, *, rows):
    i = pl.program_id(0)

    def row_copy(j, src_row):
        return pltpu.make_async_copy(x_hbm.at[pl.ds(src_row, 1), :], buf.at[pl.ds(j, 1), :], sem)

    @pl.when(i < nused_ref[0])
    def _():
        def issue(j, carry):
            row_copy(j, idx_ref[0, 0, j]).start()
            return carry
        lax.fori_loop(0, rows, issue, 0)

        def drain(j, carry):
            row_copy(j, 0).wait()
            return carry
        lax.fori_loop(0, rows, drain, 0)
        o_ref[...] = buf[...].astype(o_ref.dtype)

    @pl.when(i >= nused_ref[0])
    def _():
        o_ref[...] = jnp.zeros_like(o_ref)


def gather_rows(x, slot_rows, n_used, rows_per_step):
    P = slot_rows.shape[0]
    D = x.shape[1]
    nb = P // rows_per_step
    return pl.pallas_call(
        functools.partial(_gather_kernel, rows=rows_per_step),
        grid_spec=pltpu.PrefetchScalarGridSpec(
            num_scalar_prefetch=1, grid=(nb,),
            in_specs=[pl.BlockSpec((1, 1, rows_per_step), lambda i, nu: (i, 0, 0), memory_space=pltpu.SMEM),
                      pl.BlockSpec(memory_space=pl.ANY)],
            out_specs=pl.BlockSpec((rows_per_step, D), lambda i, nu: (i, 0)),
            scratch_shapes=[pltpu.VMEM((rows_per_step, D), x.dtype), pltpu.SemaphoreType.DMA(())]),
        out_shape=jax.ShapeDtypeStruct((P, D), BF16),
        compiler_params=_cparams(("arbitrary",)), name="moe_gather",
    )(n_used, slot_rows.reshape(nb, 1, rows_per_step), x)


def _expert_up_kernel(be_ref, nused_ref, x_ref, wg_ref, wu_ref, bg_ref, bu_ref, o_ref, wg_s, wu_s):
    i = pl.program_id(1)
    changed = jnp.logical_or(i == 0, be_ref[i] != be_ref[jnp.maximum(i - 1, 0)])

    @pl.when(changed)
    def _():
        wg_s[...] = wg_ref[0].astype(BF16)
        wu_s[...] = wu_ref[0].astype(BF16)

    @pl.when(i < nused_ref[0])
    def _():
        x = x_ref[...]
        g = jnp.minimum(_dot(x, wg_s[...]) + bg_ref[0], SWIGLU_LIMIT)
        u = jnp.clip(_dot(x, wu_s[...]) + bu_ref[0], -SWIGLU_LIMIT, SWIGLU_LIMIT)
        o_ref[...] = ((u + 1.0) * (g * jax.nn.sigmoid(SWIGLU_ALPHA * g))).astype(o_ref.dtype)

    @pl.when(i >= nused_ref[0])
    def _():
        o_ref[...] = jnp.zeros_like(o_ref)


def expert_up(xs, block_e, n_used, w_gate_up, b_gate_up, tm):
    P, D = xs.shape
    E, _, F2 = w_gate_up.shape
    Fd = F2 // 2
    tf = _pick(Fd, (512, 256, 128))
    nf = Fd // tf
    return pl.pallas_call(
        _expert_up_kernel,
        grid_spec=pltpu.PrefetchScalarGridSpec(
            num_scalar_prefetch=2, grid=(nf, P // tm),
            in_specs=[pl.BlockSpec((tm, D), lambda j, i, be, nu: (i, 0)),
                      pl.BlockSpec((1, D, tf), lambda j, i, be, nu: (be[i], 0, j)),
                      pl.BlockSpec((1, D, tf), lambda j, i, be, nu: (be[i], 0, nf + j)),
                      pl.BlockSpec((1, 1, tf), lambda j, i, be, nu: (be[i], 0, j)),
                      pl.BlockSpec((1, 1, tf), lambda j, i, be, nu: (be[i], 0, nf + j))],
            out_specs=pl.BlockSpec((tm, tf), lambda j, i, be, nu: (i, j)),
            scratch_shapes=[pltpu.VMEM((D, tf), BF16), pltpu.VMEM((D, tf), BF16)]),
        out_shape=jax.ShapeDtypeStruct((P, Fd), BF16),
        compiler_params=_cparams(("arbitrary", "arbitrary")), name="moe_up",
    )(block_e, n_used, xs, w_gate_up, w_gate_up, b_gate_up.reshape(E, 1, F2), b_gate_up.reshape(E, 1, F2))


def _expert_down_kernel(be_ref, nused_ref, x_ref, w_ref, b_ref, o_ref, w_s):
    i = pl.program_id(1)
    changed = jnp.logical_or(i == 0, be_ref[i] != be_ref[jnp.maximum(i - 1, 0)])

    @pl.when(changed)
    def _():
        w_s[...] = w_ref[0].astype(BF16)

    @pl.when(i < nused_ref[0])
    def _():
        o_ref[...] = _dot(x_ref[...], w_s[...]) + b_ref[0]

    @pl.when(i >= nused_ref[0])
    def _():
        o_ref[...] = jnp.zeros_like(o_ref)


def expert_down(hmid, block_e, n_used, w_down, b_down, tm):
    P, Fd = hmid.shape
    E, _, D = w_down.shape
    tn = _pick(D, (1024, 512, 256, 128))
    return pl.pallas_call(
        _expert_down_kernel,
        grid_spec=pltpu.PrefetchScalarGridSpec(
            num_scalar_prefetch=2, grid=(D // tn, P // tm),
            in_specs=[pl.BlockSpec((tm, Fd), lambda j, i, be, nu: (i, 0)),
                      pl.BlockSpec((1, Fd, tn), lambda j, i, be, nu: (be[i], 0, j)),
                      pl.BlockSpec((1, 1, tn), lambda j, i, be, nu: (be[i], 0, j))],
            out_specs=pl.BlockSpec((tm, tn), lambda j, i, be, nu: (i, j)),
            scratch_shapes=[pltpu.VMEM((Fd, tn), BF16)]),
        out_shape=jax.ShapeDtypeStruct((P, D), F32),
        compiler_params=_cparams(("arbitrary", "arbitrary")), name="moe_down",
    )(block_e, n_used, hmid, w_down, b_down.reshape(E, 1, D))


def _combine_kernel(pos_ref, y_hbm, wt_ref, h_ref, g_ref, b_ref, o_ref, obf_ref, buf, sem, *, tokens, alpha):
    def row_copy(j, k, src_row):
        return pltpu.make_async_copy(y_hbm.at[pl.ds(src_row, 1), :], buf.at[k, pl.ds(j, 1), :], sem)

    def issue(j, carry):
        for k in range(TOP_K):
            row_copy(j, k, pos_ref[0, 0, j * TOP_K + k]).start()
        return carry
    lax.fori_loop(0, tokens, issue, 0)

    def drain(j, carry):
        for k in range(TOP_K):
            row_copy(j, k, 0).wait()
        return carry
    lax.fori_loop(0, tokens, drain, 0)

    acc = alpha * h_ref[...]
    for k in range(TOP_K):
        acc += wt_ref[:, k:k + 1] * buf[k]
    y = _layer_norm(acc, g_ref[...], b_ref[...])
    o_ref[...] = y
    obf_ref[...] = y.astype(BF16)


def combine_ln(y_slots, slot_of, weights, h, g, b, alpha):
    R, D = h.shape
    tt = _pick(R, (128,))
    nb = R // tt
    row = pl.BlockSpec((tt, D), lambda i: (i, 0))
    vec = pl.BlockSpec((1, D), lambda i: (0, 0))
    return pl.pallas_call(
        functools.partial(_combine_kernel, tokens=tt, alpha=alpha), grid=(nb,),
        in_specs=[pl.BlockSpec((1, 1, tt * TOP_K), lambda i: (i, 0, 0), memory_space=pltpu.SMEM),
                  pl.BlockSpec(memory_space=pl.ANY),
                  pl.BlockSpec((tt, LANES), lambda i: (i, 0)), row, vec, vec],
        out_specs=[row, row],
        out_shape=[jax.ShapeDtypeStruct((R, D), F32), jax.ShapeDtypeStruct((R, D), BF16)],
        scratch_shapes=[pltpu.VMEM((TOP_K, tt, D), F32), pltpu.SemaphoreType.DMA(())],
        compiler_params=_cparams(("arbitrary",)), name="moe_combine",
    )(slot_of.reshape(nb, 1, tt * TOP_K), y_slots, weights, h, g.reshape(1, D), b.reshape(1, D))


def moe_ffn_ln(h, h_bf, p, l, alpha, block_rows):
    R, D = h.shape
    E = p['w_router'].shape[-1]
    wts, ids = router(h_bf, p['w_router'][l], p['b_router'][l])
    flat_e = ids[:, :TOP_K].reshape(-1)
    n_assign = R * TOP_K
    order = jnp.argsort(flat_e)
    se = flat_e[order]
    counts = jnp.zeros((E,), jnp.int32).at[flat_e].add(1)
    padded = (counts + block_rows - 1) // block_rows * block_rows
    pad_end = jnp.cumsum(padded)
    pad_start = pad_end - padded
    start = jnp.cumsum(counts) - counts
    dest = pad_start[se] + jnp.arange(n_assign, dtype=jnp.int32) - start[se]
    n_blocks = n_assign // block_rows + E
    P = n_blocks * block_rows
    slot_rows = jnp.zeros((P,), jnp.int32).at[dest].set((order // TOP_K).astype(jnp.int32))
    slot_of = jnp.zeros((n_assign,), jnp.int32).at[order].set(dest.astype(jnp.int32))
    block_e = jnp.minimum(jnp.searchsorted(pad_end, jnp.arange(n_blocks, dtype=jnp.int32) * block_rows, side='right'),
                          E - 1).astype(jnp.int32)
    n_used = (pad_end[-1] // block_rows).astype(jnp.int32).reshape(1)

    xs = gather_rows(h, slot_rows, n_used, block_rows)
    hmid = expert_up(xs, block_e, n_used, p['w_gate_up'][l], p['b_gate_up'][l], block_rows)
    ys = expert_down(hmid, block_e, n_used, p['w_down'][l], p['b_down'][l], block_rows)
    return combine_ln(ys, slot_of, wts, h, p['ln2_g'][l], p['ln2_b'][l], alpha)


def _pack_w_in(w_in, D):
    splits = (512, 512, 512, MLA_HEADS * (MLA_NOPE + MLA_ROPE), MLA_KV_RANK, MLA_ROPE,
              512, 256, 256, 512, 512, 512, 512, ML_N_GATES * ML_HEADS, N_BRANCH * D)
    offs = np.concatenate([[0], np.cumsum(splits)])
    (na_q, na_k, na_v, mla_q, mla_ckv, mla_kr, g_q, g_k, g_v, m_q, m_k, m_v, m_o, m_g, gates) = [
        w_in[:, offs[i]:offs[i + 1]] for i in range(len(splits))]
    z = lambda n: jnp.zeros((D, n), w_in.dtype)
    pack_a = jnp.concatenate([na_q, na_k, na_v, m_q, m_k, m_v, g_v, z(A_COLS - A_G_V - 256)], axis=1)
    mla_heads = []
    for h in range(MLA_HEADS):
        mla_heads += [mla_q[:, h * 192:(h + 1) * 192], z(MLA_QK_PAD - 192)]
    misc = jnp.concatenate([mla_kr, m_g, z(LANES - MLA_ROPE - ML_N_GATES * ML_HEADS)], axis=1)
    pack_f = jnp.concatenate(mla_heads + [g_q, m_o, mla_ckv, g_k, misc, z(F_GATES - F_MISC - LANES), gates], axis=1)
    assert pack_a.shape[1] == A_COLS and pack_f.shape[1] == F_GATES + N_BRANCH * D
    return pack_a.astype(BF16), pack_f.astype(BF16)


def _pack_w_ukv(w_ukv):
    rank = w_ukv.shape[0]
    w = jnp.zeros((rank + LANES, MLA_HEADS * (MLA_QK_PAD + MLA_VD)), F32)
    eye = jnp.eye(MLA_ROPE, dtype=F32)
    for h in range(MLA_HEADS):
        blk = w_ukv[:, h * (MLA_NOPE + MLA_VD):(h + 1) * (MLA_NOPE + MLA_VD)]
        w = w.at[:rank, h * MLA_QK_PAD:h * MLA_QK_PAD + MLA_NOPE].set(blk[:, :MLA_NOPE])
        w = w.at[rank:rank + MLA_ROPE, h * MLA_QK_PAD + MLA_NOPE:h * MLA_QK_PAD + MLA_NOPE + MLA_ROPE].set(eye)
        v0 = MLA_HEADS * MLA_QK_PAD + h * MLA_VD
        w = w.at[:rank, v0:v0 + MLA_VD].set(blk[:, MLA_NOPE:])
    return w.astype(BF16)


def _inv_freq(n_dims):
    return 1.0 / (ROPE_THETA ** (jnp.arange(0, n_dims, 2, dtype=F32) / n_dims))


def _rope_tables(groups, R):
    pos, row, col = [], [], []
    for g in groups:
        t = jnp.arange(g.seq)
        pos.append(jnp.tile((N_META + t).astype(F32), g.batch))
        row.append(jnp.tile((t // GRID_W).astype(F32), g.batch))
        col.append(jnp.tile((t % GRID_W).astype(F32), g.batch))
    for g in groups:
        m = jnp.arange(META_PAD)
        pos.append(jnp.tile(m.astype(F32), g.batch))
        row.append(jnp.zeros((g.batch * META_PAD,), F32))
        col.append(jnp.zeros((g.batch * META_PAD,), F32))
    pos, row, col = jnp.concatenate(pos), jnp.concatenate(row), jnp.concatenate(col)
    assert pos.shape[0] == R
    ang_seq = pos[:, None] * _inv_freq(MLA_ROPE)[None, :]
    f_ax = _inv_freq(GQA_HD // 2)
    ang_row, ang_col = row[:, None] * f_ax[None, :], col[:, None] * f_ax[None, :]
    ones, zeros = jnp.ones((R, 64), F32), jnp.zeros((R, 64), F32)
    cs, sn = jnp.cos(ang_seq), jnp.sin(ang_seq)
    mla_c = jnp.concatenate([cs, cs, ones], axis=1)
    mla_s = jnp.concatenate([-sn, sn, zeros], axis=1)
    cr, sr, cc, sc = jnp.cos(ang_row), jnp.sin(ang_row), jnp.cos(ang_col), jnp.sin(ang_col)
    gqa_c = jnp.concatenate([cr, cr, cc, cc], axis=1)
    gqa_s = jnp.concatenate([-sr, sr, -sc, sc], axis=1)
    return mla_c, mla_s, gqa_c, gqa_s


def kernel(x_prompt, x_sample, meta_tokens, ln_emb_g, ln_emb_b, w_in, nat_rpb, mla_kv_norm, mla_w_ukv,
           gqa_q_norm, gqa_k_norm, mlstm_gate_bias, mlstm_out_norm, w_branch, w_out, ln1_g, ln1_b,
           w_router, b_router, w_gate_up, b_gate_up, w_down, b_down, ln2_g, ln2_b):
    depth = w_in.shape[0]
    D = x_prompt.shape[-1]
    Bp, Tp, _ = x_prompt.shape
    Bs, Ts, _ = x_sample.shape
    n_real = Bp * Tp + Bs * Ts
    groups = (Group(Bp, Tp, 0, n_real), Group(Bs, Ts, Bp * Tp, n_real + Bp * META_PAD))
    R = n_real + (Bp + Bs) * META_PAD

    meta_blk = jnp.zeros((META_PAD, D), F32).at[:N_META].set(meta_tokens.astype(F32))
    x = jnp.concatenate([x_prompt.reshape(Bp * Tp, D), x_sample.reshape(Bs * Ts, D),
                         jnp.tile(meta_blk, (Bp + Bs, 1))], axis=0)
    h, h_bf = emb_layer_norm(x, ln_emb_g, ln_emb_b)

    mla_c, mla_s, gqa_c, gqa_s = _rope_tables(groups, R)
    alpha = (2.0 * depth) ** 0.25
    moe = {'w_router': w_router, 'b_router': b_router, 'w_gate_up': w_gate_up, 'b_gate_up': b_gate_up,
           'w_down': w_down, 'b_down': b_down, 'ln2_g': ln2_g, 'ln2_b': ln2_b}
    moe_rows = 512 if D >= 1024 else 128

    for l in range(depth):
        pack_a, pack_f = _pack_w_in(w_in[l], D)
        proj_a = matmul(h_bf, pack_a, BF16, "in_proj_a")
        proj_f = matmul(h_bf, pack_f, F32, "in_proj_f")

        y_a = neighbourhood_attention(proj_a, _na_bias_table(nat_rpb[l]), groups)

        q_mla, ck = mla_prep(proj_f, mla_c, mla_s, mla_kv_norm[l])
        kv = matmul(ck, _pack_w_ukv(mla_w_ukv[l]), BF16, "mla_kv_up")
        y_b = dense_attention(q_mla, 0, kv, 0, kv, (MLA_HEADS * MLA_QK_PAD) // (MLA_HEADS * MLA_VD), groups,
                              heads=MLA_HEADS, kv_heads=MLA_HEADS, dq=MLA_QK_PAD, dv=MLA_VD,
                              scale=(MLA_NOPE + MLA_ROPE) ** -0.5, name="mla")

        q_g, k_g = gqa_prep(proj_f, gqa_c, gqa_s, gqa_q_norm[l], gqa_k_norm[l])
        y_c = dense_attention(q_g, 0, k_g, 0, proj_a, A_G_V // (GQA_KV_HEADS * GQA_HD), groups,
                              heads=GQA_HEADS, kv_heads=GQA_KV_HEADS, dq=GQA_HD, dv=GQA_HD,
                              scale=GQA_HD ** -0.5, name="gqa")

        ng = ML_N_GATES * ML_HEADS
        gate_cols = proj_f[:, F_MISC + MISC_MG:F_MISC + MISC_MG + ng]
        hf, hb = mlstm_scan(proj_a, gate_cols.T, gate_cols, mlstm_gate_bias[l], groups)

        merged = merge_branches(y_a, y_b, y_c, hf, hb, proj_f, mlstm_out_norm[l], w_branch[l].astype(BF16))
        h, h_bf = out_proj_ln(merged, w_out[l].astype(BF16), h, ln1_g[l], ln1_b[l], alpha)
        h, h_bf = moe_ffn_ln(h, h_bf, moe, l, alpha, moe_rows)

    y_prompt = h[:Bp * Tp].reshape(Bp, Tp, D)
    y_sample = h[Bp * Tp:n_real].reshape(Bs, Ts, D)
    return (y_prompt, y_sample)
```
